```python
import math
import jax, jax.numpy as jnp
from jax import lax
import numpy as np

D_MODEL = 2048
BATCH = 2
SEQ = 4096
DEPTH = 1

CHUNK = 64
Q_BLOCK = 2 * CHUNK
HEAD_DIM = 128
SB_HEADS = 8
FOX_HEADS = 8
SB_WIDTH = SB_HEADS * HEAD_DIM
FOX_WIDTH = FOX_HEADS * HEAD_DIM
IN_SPLITS = [SB_WIDTH, SB_WIDTH, SB_WIDTH,
             FOX_WIDTH, FOX_WIDTH, FOX_WIDTH,
             FOX_HEADS,
             D_MODEL, D_MODEL]
IN_WIDTH = sum(IN_SPLITS)
PEER_HEADS = 8
N_KEYS = 128
N_EXPERTS = N_KEYS * N_KEYS
PEER_TOPK = 16
D_KEY = 256
HALF_KEY = D_KEY // 2
PEER_BLOCK = 128
EPS = 1e-6
NEG_INF = -1e30

kernel_name = "hybrid_sb_fox_peer_block"


def rms_norm(x, gain):
    x32 = x.astype(jnp.float32)
    y = x32 * lax.rsqrt(jnp.mean(x32 * x32, axis=-1, keepdims=True) + EPS)
    return (y * gain.astype(jnp.float32)).astype(x.dtype)


def to_heads(t, n_heads):
    b, s, _ = t.shape
    return t.reshape(b, s, n_heads, HEAD_DIM).transpose(0, 2, 1, 3)


def from_heads(t):
    b, h, s, d = t.shape
    return t.transpose(0, 2, 1, 3).reshape(b, s, h * d)


def stick_breaking_attention(q, k, v):
    s_len = q.shape[2]
    scale = 1.0 / math.sqrt(HEAD_DIM)
    outs = []
    for blk in range(s_len // Q_BLOCK):
        t0, t1 = blk * Q_BLOCK, (blk + 1) * Q_BLOCK
        qb = q[:, :, t0:t1].astype(jnp.float32)
        kb = k[:, :, :t1].astype(jnp.float32)
        vb = v[:, :, :t1].astype(jnp.float32)
        z = jnp.einsum('bhqd,bhkd->bhqk', qb, kb) * scale
        t_idx = jnp.arange(t0, t1)[:, None]
        s_idx = jnp.arange(t1)[None, :]
        strict = s_idx < t_idx
        log_1m_beta = jnp.where(strict, jax.nn.log_sigmoid(-z), 0.0)
        suffix = lax.cumsum(log_1m_beta, axis=3, reverse=True) - log_1m_beta
        weights = jnp.where(strict, jnp.exp(jax.nn.log_sigmoid(z) + suffix), 0.0)
        outs.append(jnp.einsum('bhqk,bhkd->bhqd', weights, vb))
    return jnp.concatenate(outs, axis=2).astype(q.dtype)


def forgetting_attention(q, k, v, log_f):
    s_len = q.shape[2]
    scale = 1.0 / math.sqrt(HEAD_DIM)
    cum_f = jnp.cumsum(log_f.astype(jnp.float32), axis=-1)
    outs = []
    for blk in range(s_len // Q_BLOCK):
        t0, t1 = blk * Q_BLOCK, (blk + 1) * Q_BLOCK
        qb = q[:, :, t0:t1].astype(jnp.float32)
        kb = k[:, :, :t1].astype(jnp.float32)
        vb = v[:, :, :t1].astype(jnp.float32)
        logits = jnp.einsum('bhqd,bhkd->bhqk', qb, kb) * scale
        logits = logits + cum_f[:, :, t0:t1, None] - cum_f[:, :, None, :t1]
        causal = jnp.arange(t1)[None, :] <= jnp.arange(t0, t1)[:, None]
        probs = jax.nn.softmax(jnp.where(causal, logits, NEG_INF), axis=-1)
        outs.append(jnp.einsum('bhqk,bhkd->bhqd', probs, vb))
    return jnp.concatenate(outs, axis=2).astype(q.dtype)


def hybrid_mixer(h, w_in, b_forget, w_branch_a, w_branch_b, w_out):
    proj = h @ w_in
    cuts = list(np.cumsum(IN_SPLITS)[:-1])
    q_a, k_a, v_a, q_b, k_b, v_b, f_logit, g_a, g_b = jnp.split(proj, cuts, axis=-1)
    o_a = stick_breaking_attention(to_heads(q_a, SB_HEADS), to_heads(k_a, SB_HEADS),
                                   to_heads(v_a, SB_HEADS))
    log_f = jax.nn.log_sigmoid(f_logit.astype(jnp.float32) + b_forget.astype(jnp.float32))
    log_f = log_f.transpose(0, 2, 1)
    o_b = forgetting_attention(to_heads(q_b, FOX_HEADS), to_heads(k_b, FOX_HEADS),
                               to_heads(v_b, FOX_HEADS), log_f)
    merged = (jax.nn.sigmoid(g_a) * (from_heads(o_a) @ w_branch_a)
              + jax.nn.sigmoid(g_b) * (from_heads(o_b) @ w_branch_b))
    return merged @ w_out


def peer_ffn(h, w_query, sub_keys, expert_u, expert_v):
    b, s, d = h.shape
    n_tok = b * s
    ht = h.reshape(n_tok, d)
    q = (ht @ w_query).astype(jnp.float32).reshape(n_tok, PEER_HEADS, 2, HALF_KEY)
    scores = jnp.einsum('thpc,hpnc->thpn', q, sub_keys.astype(jnp.float32))
    s1, i1 = lax.top_k(scores[:, :, 0], PEER_TOPK)
    s2, i2 = lax.top_k(scores[:, :, 1], PEER_TOPK)
    cand_score = (s1[..., :, None] + s2[..., None, :]).reshape(n_tok, PEER_HEADS, PEER_TOPK * PEER_TOPK)
    cand_id = (i1[..., :, None] * N_KEYS + i2[..., None, :]).reshape(n_tok, PEER_HEADS, PEER_TOPK * PEER_TOPK)
    top_score, pos = lax.top_k(cand_score, PEER_TOPK)
    expert_ids = jnp.take_along_axis(cand_id, pos, axis=-1)
    gates = jax.nn.softmax(top_score, axis=-1)

    n_blk = n_tok // PEER_BLOCK

    def expert_block(args):
        xb, idb, gb = args
        u_sel = expert_u[idb]
        act = jax.nn.gelu(jnp.einsum('thkd,td->thk', u_sel, xb).astype(jnp.float32))
        w = (gb * act).astype(xb.dtype)
        v_sel = expert_v[idb]
        return jnp.einsum('thk,thkd->td', w, v_sel)

    out = lax.map(expert_block, (ht.reshape(n_blk, PEER_BLOCK, d),
                                 expert_ids.reshape(n_blk, PEER_BLOCK, PEER_HEADS, PEER_TOPK),
                                 gates.reshape(n_blk, PEER_BLOCK, PEER_HEADS, PEER_TOPK)))
    return out.reshape(b, s, d)


def setup_inputs(seed: int = 0) -> dict:
    key = jax.random.key(seed)
    ks = jax.random.split(key, 14)
    f32 = jnp.float32
    x = jax.random.normal(ks[0], (BATCH, SEQ, D_MODEL), f32)
    norm_mix_gain = 1.0 + 0.02 * jax.random.normal(ks[1], (DEPTH, D_MODEL), f32)
    w_in = jax.random.normal(ks[2], (DEPTH, D_MODEL, IN_WIDTH), f32) * D_MODEL ** -0.5
    b_forget = jax.random.uniform(ks[3], (DEPTH, FOX_HEADS), f32, minval=1.0, maxval=4.0)
    w_branch_a = jax.random.normal(ks[4], (DEPTH, SB_WIDTH, D_MODEL), f32) * SB_WIDTH ** -0.5
    w_branch_b = jax.random.normal(ks[5], (DEPTH, FOX_WIDTH, D_MODEL), f32) * FOX_WIDTH ** -0.5
    w_out = jax.random.normal(ks[6], (DEPTH, D_MODEL, D_MODEL), f32) * D_MODEL ** -0.5
    norm_ffn_gain = 1.0 + 0.02 * jax.random.normal(ks[7], (DEPTH, D_MODEL), f32)
    w_query = jax.random.normal(ks[8], (DEPTH, D_MODEL, PEER_HEADS * D_KEY), f32) * D_MODEL ** -0.5
    sub_keys = jax.random.normal(ks[9], (DEPTH, PEER_HEADS, 2, N_KEYS, HALF_KEY), f32) * HALF_KEY ** -0.5
    expert_u = jax.random.normal(ks[10], (DEPTH, N_EXPERTS, D_MODEL), f32) * D_MODEL ** -0.5
    expert_v = jax.random.normal(ks[11], (DEPTH, N_EXPERTS, D_MODEL), f32) * PEER_HEADS ** -0.5
    norm_final_gain = 1.0 + 0.02 * jax.random.normal(ks[12], (D_MODEL,), f32)
    return {"x": x, "norm_mix_gain": norm_mix_gain, "w_in": w_in, "b_forget": b_forget,
            "w_branch_a": w_branch_a, "w_branch_b": w_branch_b, "w_out": w_out,
            "norm_ffn_gain": norm_ffn_gain, "w_query": w_query, "sub_keys": sub_keys,
            "expert_u": expert_u, "expert_v": expert_v, "norm_final_gain": norm_final_gain}


def reference(x, norm_mix_gain, w_in, b_forget, w_branch_a, w_branch_b, w_out,
              norm_ffn_gain, w_query, sub_keys, expert_u, expert_v, norm_final_gain):
    for layer in range(DEPTH):
        h = rms_norm(x, norm_mix_gain[layer])
        x = x + hybrid_mixer(h, w_in[layer], b_forget[layer], w_branch_a[layer],
                             w_branch_b[layer], w_out[layer])
        h = rms_norm(x, norm_ffn_gain[layer])
        x = x + peer_ffn(h, w_query[layer], sub_keys[layer], expert_u[layer], expert_v[layer])
    return rms_norm(x, norm_final_gain)
```

```python
import functools
import math

import numpy as np
import jax
import jax.numpy as jnp
from jax import lax
from jax.experimental import pallas as pl
from jax.experimental.pallas import tpu as pltpu

EPS = 1e-6
NEG_INF = -1e30
HEAD_DIM = 128
N_KEYS = 128
PEER_TOPK = 16
LANES = 128
ATT_BLOCK = 256
VMEM_LIMIT = 48 * 1024 * 1024

F32 = jnp.float32
BF16 = jnp.bfloat16


def _params(*sem):
    return pltpu.CompilerParams(dimension_semantics=sem, vmem_limit_bytes=VMEM_LIMIT)


def _log_sigmoid(z):
    return jnp.minimum(z, 0.0) - jnp.log1p(jnp.exp(-jnp.abs(z)))


def _split3(x):
    hi = x.astype(BF16)
    r1 = x - hi.astype(F32)
    mid = r1.astype(BF16)
    lo = (r1 - mid.astype(F32)).astype(BF16)
    return hi, mid, lo


def _norm_kernel(*refs, has_res, emit_sum):
    it = iter(refs)
    x_ref = next(it)
    r_ref = next(it) if has_res else None
    g_ref = next(it)
    s_ref = next(it) if emit_sum else None
    h_ref = next(it)
    x = x_ref[...]
    if has_res:
        x = x + r_ref[...]
    if emit_sum:
        s_ref[...] = x
    ms = jnp.mean(x * x, axis=-1, keepdims=True)
    y = x * lax.rsqrt(ms + EPS)
    h_ref[...] = (y * g_ref[...]).astype(h_ref.dtype)


def _rms_norm(x, gain, res=None, emit_sum=False, out_dtype=BF16, tm=512):
    t, d = x.shape
    row = pl.BlockSpec((tm, d), lambda i: (i, 0))
    in_specs = [row] + ([row] if res is not None else []) + [pl.BlockSpec((1, d), lambda i: (0, 0))]
    args = [x] + ([res] if res is not None else []) + [gain.reshape(1, d)]
    out_shape = [jax.ShapeDtypeStruct((t, d), out_dtype)]
    out_specs = [row]
    if emit_sum:
        out_shape = [jax.ShapeDtypeStruct((t, d), F32)] + out_shape
        out_specs = [row, row]
    out = pl.pallas_call(
        functools.partial(_norm_kernel, has_res=res is not None, emit_sum=emit_sum),
        grid=(t // tm,), in_specs=in_specs, out_specs=out_specs, out_shape=out_shape,
        compiler_params=_params("parallel"), name="rms_norm")(*args)
    return out if emit_sum else out[0]


def _mm_kernel(*refs, epilogue):
    if epilogue == "residual":
        a_ref, w_ref, r_ref, o_ref = refs
    else:
        a_ref, w_ref, o_ref = refs
    acc = jnp.dot(a_ref[...], w_ref[...].astype(BF16), preferred_element_type=F32)
    if epilogue == "sigmoid":
        acc = jax.nn.sigmoid(acc)
    elif epilogue == "residual":
        acc = acc + r_ref[...]
    o_ref[...] = acc.astype(o_ref.dtype)


def _matmul(a, w, col0, ncols, out_dtype, epilogue="none", res=None, tm=1024, tn=512):
    t, k = a.shape
    tn = min(tn, ncols)
    assert col0 % tn == 0 and ncols % tn == 0 and t % tm == 0
    cb = col0 // tn
    in_specs = [pl.BlockSpec((tm, k), lambda i, j: (i, 0)),
                pl.BlockSpec((k, tn), lambda i, j: (0, cb + j))]
    args = [a, w]
    if epilogue == "residual":
        in_specs.append(pl.BlockSpec((tm, tn), lambda i, j: (i, j)))
        args.append(res)
    return pl.pallas_call(
        functools.partial(_mm_kernel, epilogue=epilogue),
        grid=(t // tm, ncols // tn), in_specs=in_specs,
        out_specs=pl.BlockSpec((tm, tn), lambda i, j: (i, j)),
        out_shape=jax.ShapeDtypeStruct((t, ncols), out_dtype),
        compiler_params=_params("parallel", "arbitrary"), name="matmul_" + epilogue)(*args)


def _fcum_kernel(fs_ref, b_ref, f_ref, carry_ref, *, ch):
    @pl.when(pl.program_id(1) == 0)
    def _():
        carry_ref[...] = jnp.zeros_like(carry_ref)

    lf = _log_sigmoid(fs_ref[...] + b_ref[...])
    row = lax.broadcasted_iota(jnp.int32, (ch, ch), 0)
    col = lax.broadcasted_iota(jnp.int32, (ch, ch), 1)
    tri = jnp.where(col <= row, 1.0, 0.0).astype(BF16)
    hi, mid, lo = _split3(lf)
    cs = (jnp.dot(tri, hi, preferred_element_type=F32) + jnp.dot(tri, mid, preferred_element_type=F32)
          + jnp.dot(tri, lo, preferred_element_type=F32))
    fc = cs + carry_ref[0:1, :]
    f_ref[...] = fc
    carry_ref[...] = jnp.broadcast_to(fc[ch - 1:ch, :], carry_ref.shape)


def _forget_cumsum(fslab, bias_row, batch, seq, ch=256):
    nch = seq // ch
    return pl.pallas_call(
        functools.partial(_fcum_kernel, ch=ch),
        grid=(batch, nch),
        in_specs=[pl.BlockSpec((ch, LANES), lambda b, c: (b * nch + c, 0)),
                  pl.BlockSpec((1, LANES), lambda b, c: (0, 0))],
        out_specs=pl.BlockSpec((ch, LANES), lambda b, c: (b * nch + c, 0)),
        out_shape=jax.ShapeDtypeStruct((batch * seq, LANES), F32),
        scratch_shapes=[pltpu.VMEM((8, LANES), F32)],
        compiler_params=_params("arbitrary", "arbitrary"), name="forget_cumsum")(fslab, bias_row)


def _dot_nt(a, b):
    return lax.dot_general(a, b, (((1,), (1,)), ((), ())), preferred_element_type=F32)


def _dot_tn(a, b):
    return lax.dot_general(a, b, (((0,), (0,)), ((), ())), preferred_element_type=F32)


def _sb_kernel(q_ref, k_ref, v_ref, o_ref, acc_ref, *, seq, blk):
    scale = 1.0 / math.sqrt(HEAD_DIM)
    key_i = lax.broadcasted_iota(jnp.int32, (blk, blk), 0)
    qry_i = lax.broadcasted_iota(jnp.int32, (blk, blk), 1)
    strict = key_i < qry_i
    after = jnp.where(qry_i > key_i, 1.0, 0.0).astype(BF16)

    def block(qb, j, carry, masked):
        off = pl.multiple_of(j * blk, blk)
        kb = k_ref[pl.ds(off, blk), :]
        vb = v_ref[pl.ds(off, blk), :]
        z = _dot_nt(kb, qb) * scale
        ls = _log_sigmoid(z)
        l1m = ls - z
        if masked:
            l1m = jnp.where(strict, l1m, 0.0)
        hi = l1m.astype(BF16)
        lo = (l1m - hi.astype(F32)).astype(BF16)
        suf = jnp.dot(after, hi, preferred_element_type=F32) + jnp.dot(after, lo, preferred_element_type=F32)
        w = jnp.exp(ls + suf + carry)
        if masked:
            w = jnp.where(strict, w, 0.0)
        acc_ref[...] += _dot_tn(vb, w.astype(BF16))
        return carry + suf[0:1, :] + l1m[0:1, :]

    def q_loop(i, _):
        qoff = pl.multiple_of(i * blk, blk)
        qb = q_ref[pl.ds(qoff, blk), :]
        acc_ref[...] = jnp.zeros_like(acc_ref)
        carry = block(qb, i, jnp.zeros((1, blk), F32), True)
        lax.fori_loop(0, i, lambda jj, c: block(qb, i - 1 - jj, c, False), carry)
        o_ref[pl.ds(qoff, blk), :] = acc_ref[...].T.astype(o_ref.dtype)
        return 0

    lax.fori_loop(0, seq // blk, q_loop, 0)


def _fox_kernel(q_ref, k_ref, v_ref, fb_ref, fr_ref, o_ref, acc_ref, frep_ref, *, seq, blk):
    scale = 1.0 / math.sqrt(HEAD_DIM)
    h = pl.program_id(1)
    key_i = lax.broadcasted_iota(jnp.int32, (blk, blk), 0)
    qry_i = lax.broadcasted_iota(jnp.int32, (blk, blk), 1)
    causal = key_i <= qry_i

    sel = jnp.where(lax.broadcasted_iota(jnp.int32, (LANES, LANES), 0) == h, 1.0, 0.0).astype(BF16)

    def rep_loop(c, _):
        off = pl.multiple_of(c * blk, blk)
        hi, mid, lo = _split3(fb_ref[pl.ds(off, blk), :])
        frep_ref[pl.ds(off, blk), :] = (jnp.dot(hi, sel, preferred_element_type=F32)
                                        + jnp.dot(mid, sel, preferred_element_type=F32)
                                        + jnp.dot(lo, sel, preferred_element_type=F32))
        return 0

    lax.fori_loop(0, seq // blk, rep_loop, 0)

    def block(qb, fq, j, carry, masked):
        m, l = carry
        off = pl.multiple_of(j * blk, blk)
        kb = k_ref[pl.ds(off, blk), :]
        vb = v_ref[pl.ds(off, blk), :]
        fk = frep_ref[pl.ds(off, blk), :]
        fk = jnp.concatenate([fk] * (blk // LANES), axis=1)
        logits = _dot_nt(kb, qb) * scale + fq - fk
        if masked:
            logits = jnp.where(causal, logits, NEG_INF)
        m_new = jnp.maximum(m, jnp.max(logits, axis=0, keepdims=True))
        alpha = jnp.exp(m - m_new)
        p = jnp.exp(logits - m_new)
        l = alpha * l + jnp.sum(p, axis=0, keepdims=True)
        acc_ref[...] = acc_ref[...] * alpha + _dot_tn(vb, p.astype(BF16))
        return m_new, l

    def q_loop(i, _):
        qoff = pl.multiple_of(i * blk, blk)
        qb = q_ref[pl.ds(qoff, blk), :]
        fq = fr_ref[0, 0, pl.ds(i, 1), :]
        acc_ref[...] = jnp.zeros_like(acc_ref)
        carry = (jnp.full((1, blk), NEG_INF, F32), jnp.zeros((1, blk), F32))
        carry = block(qb, fq, i, carry, True)
        _, l = lax.fori_loop(0, i, lambda jj, c: block(qb, fq, i - 1 - jj, c, False), carry)
        o_ref[pl.ds(qoff, blk), :] = (acc_ref[...] / l).T.astype(o_ref.dtype)
        return 0

    lax.fori_loop(0, seq // blk, q_loop, 0)


def _attention(qkv, col_base, n_heads, batch, seq, forget=None):
    blk = ATT_BLOCK
    t = batch * seq

    def head_spec(which):
        return pl.BlockSpec((seq, HEAD_DIM), lambda b, h: (b, col_base + which * n_heads + h))

    in_specs = [head_spec(0), head_spec(1), head_spec(2)]
    args = [qkv, qkv, qkv]
    scratch = [pltpu.VMEM((HEAD_DIM, blk), F32)]
    if forget is None:
        kern = functools.partial(_sb_kernel, seq=seq, blk=blk)
        name = "stick_breaking_attention"
    else:
        f_cum, f_rows = forget
        in_specs += [pl.BlockSpec((seq, LANES), lambda b, h: (b, 0)),
                     pl.BlockSpec((1, 1, seq // blk, blk), lambda b, h: (b, h, 0, 0))]
        args += [f_cum, f_rows]
        scratch.append(pltpu.VMEM((seq, LANES), F32))
        kern = functools.partial(_fox_kernel, seq=seq, blk=blk)
        name = "forgetting_attention"
    return pl.pallas_call(
        kern, grid=(batch, n_heads), in_specs=in_specs,
        out_specs=pl.BlockSpec((seq, HEAD_DIM), lambda b, h: (b, h)),
        out_shape=jax.ShapeDtypeStruct((t, n_heads * HEAD_DIM), BF16),
        scratch_shapes=scratch,
        compiler_params=_params("parallel", "arbitrary"), name=name)(*args)


def _merge_kernel(oa_ref, ob_ref, wa_ref, wb_ref, ga_ref, gb_ref, o_ref):
    pa = jnp.dot(oa_ref[...], wa_ref[...].astype(BF16), preferred_element_type=F32)
    pb = jnp.dot(ob_ref[...], wb_ref[...].astype(BF16), preferred_element_type=F32)
    o_ref[...] = (ga_ref[...].astype(F32) * pa + gb_ref[...].astype(F32) * pb).astype(o_ref.dtype)


def _merge(o_a, o_b, w_a, w_b, gates, d, tm=1024, tn=512):
    t, wa = o_a.shape
    wb = o_b.shape[1]
    tn = min(tn, d)
    ng = d // tn
    return pl.pallas_call(
        _merge_kernel, grid=(t // tm, ng),
        in_specs=[pl.BlockSpec((tm, wa), lambda i, j: (i, 0)),
                  pl.BlockSpec((tm, wb), lambda i, j: (i, 0)),
                  pl.BlockSpec((wa, tn), lambda i, j: (0, j)),
                  pl.BlockSpec((wb, tn), lambda i, j: (0, j)),
                  pl.BlockSpec((tm, tn), lambda i, j: (i, j)),
                  pl.BlockSpec((tm, tn), lambda i, j: (i, ng + j))],
        out_specs=pl.BlockSpec((tm, tn), lambda i, j: (i, j)),
        out_shape=jax.ShapeDtypeStruct((t, d), BF16),
        compiler_params=_params("parallel", "arbitrary"), name="branch_merge")(o_a, o_b, w_a, w_b, gates, gates)


def _candidate_layout():
    pos = []
    for b in range(16):
        pos.append(b)
    for b in range(8):
        pos.append(16 + b)
    for a in range(2, 8):
        for b in range(8):
            pos.append(16 * a + b if b < 16 // (a + 1) else -1)
    for a in range(8, 16):
        pos.append(16 * a)
    return np.asarray(pos, np.int32)


_CAND_POS = _candidate_layout()
_N_CAND = _CAND_POS.shape[0]


def _top16(s, iota):
    n = s.shape[0]
    rank = jnp.full(s.shape, float(PEER_TOPK), F32)
    vals = []
    for a in range(PEER_TOPK):
        m = jnp.max(s, axis=0, keepdims=True)
        idx = jnp.min(jnp.where(s == m, iota, n), axis=0, keepdims=True)
        hit = iota == idx
        rank = jnp.where(hit, float(a), rank)
        s = jnp.where(hit, -jnp.inf, s)
        vals.append(m)
    return rank, vals


def _select_kernel(q_ref, key_ref, pos_ref, n1_ref, r2_ref, e1_ref, e2_ref):
    tl = q_ref.shape[0]
    iota = lax.broadcasted_iota(jnp.int32, (N_KEYS, tl), 0)
    half = q_ref.shape[1] // 2
    s1 = _dot_nt(key_ref[0, 0].astype(BF16), q_ref[:, 0:half])
    s2 = _dot_nt(key_ref[0, 1].astype(BF16), q_ref[:, half:2 * half])
    rank1, v1 = _top16(s1, iota)
    rank2, v2 = _top16(s2, iota)

    v2_16 = jnp.concatenate(v2, axis=0)
    v2_8 = v2_16[0:8]
    groups = [v1[0] + v2_16, v1[1] + v2_8] + [v1[a] + v2_8 for a in range(2, 8)]
    groups.append(jnp.concatenate(v1[8:16], axis=0) + v2[0])
    pos = pos_ref[...]
    cand0 = jnp.where(pos >= 0, jnp.concatenate(groups, axis=0), -jnp.inf)
    cand = cand0
    chosen = jnp.zeros(cand.shape, F32)
    for _ in range(PEER_TOPK):
        m = jnp.max(cand, axis=0, keepdims=True)
        pmin = jnp.min(jnp.where(cand == m, pos, 4096), axis=0, keepdims=True)
        hit = pos == pmin
        chosen = jnp.where(hit, 1.0, chosen)
        cand = jnp.where(hit, -jnp.inf, cand)
    cmax = v1[0] + v2[0]
    zsum = jnp.sum(jnp.where(chosen > 0.0, jnp.exp(cand0 - cmax), 0.0), axis=0, keepdims=True)

    n_a = [jnp.sum(chosen[0:16], axis=0, keepdims=True), jnp.sum(chosen[16:24], axis=0, keepdims=True)]
    for a in range(2, 8):
        n_a.append(jnp.sum(chosen[24 + 8 * (a - 2):32 + 8 * (a - 2)], axis=0, keepdims=True))
    for a in range(8, 16):
        n_a.append(chosen[72 + a - 8:73 + a - 8])
    n1 = jnp.zeros(rank1.shape, F32)
    for a in range(PEER_TOPK):
        n1 = jnp.where(rank1 == float(a), n_a[a], n1)

    n1_ref[0] = n1
    r2_ref[0] = rank2
    e1_ref[0] = jnp.exp(s1 - v1[0])
    e2_ref[0] = jnp.exp(s2 - v2[0]) / zsum


def _peer_select(q, sub_keys, tl=256):
    t = q.shape[0]
    heads = sub_keys.shape[0]
    pos = jnp.asarray(np.broadcast_to(_CAND_POS[:, None], (_N_CAND, tl)).copy())
    out = jax.ShapeDtypeStruct((heads, N_KEYS, t), F32)
    ospec = pl.BlockSpec((1, N_KEYS, tl), lambda i, h: (h, 0, i))
    return pl.pallas_call(
        _select_kernel, grid=(t // tl, heads),
        in_specs=[pl.BlockSpec((tl, 2 * N_KEYS), lambda i, h: (i, h)),
                  pl.BlockSpec((1, 2, N_KEYS, N_KEYS), lambda i, h: (h, 0, 0, 0)),
                  pl.BlockSpec((_N_CAND, tl), lambda i, h: (0, 0))],
        out_specs=[ospec] * 4, out_shape=[out] * 4,
        compiler_params=_params("parallel", "arbitrary"), name="peer_select")(q, sub_keys, pos)


def _peer_kernel(h_ref, u_ref, v_ref, n1_ref, r2_ref, e1_ref, e2_ref, o_ref, w_ref, *, heads):
    j = pl.program_id(1)
    et, tt = w_ref.shape
    groups = et // N_KEYS

    @pl.when(j == 0)
    def _():
        o_ref[...] = jnp.zeros_like(o_ref)

    act = _dot_nt(u_ref[...].astype(BF16), h_ref[...])
    per_slab = 8 // groups
    slab = pl.multiple_of((j // per_slab) * 8, 8)
    part = j % per_slab

    def key_row(ref, hd, g, cs):
        rows = ref[hd, pl.ds(slab, 8), cs]
        out = rows[g:g + 1]
        for p in range(1, per_slab):
            out = jnp.where(part == p, rows[p * groups + g:p * groups + g + 1], out)
        return out

    for g in range(groups):
        for c in range(tt // LANES):
            cs = slice(c * LANES, (c + 1) * LANES)
            gate = jnp.zeros((N_KEYS, LANES), F32)
            for hd in range(heads):
                n1 = key_row(n1_ref, hd, g, cs)
                e1 = key_row(e1_ref, hd, g, cs)
                gate = gate + jnp.where(r2_ref[hd, :, cs] < n1, e1 * e2_ref[hd, :, cs], 0.0)
            a = act[g * N_KEYS:(g + 1) * N_KEYS, cs]
            w_ref[g * N_KEYS:(g + 1) * N_KEYS, cs] = (gate * jax.nn.gelu(a)).astype(BF16)
    o_ref[...] += _dot_tn(w_ref[...], v_ref[...].astype(BF16))


def _peer_dense(h, expert_u, expert_v, n1, r2, e1, e2, tt=512, et=512):
    t, d = h.shape
    n_exp = expert_u.shape[0]
    heads = n1.shape[0]
    mspec = pl.BlockSpec((heads, N_KEYS, tt), lambda i, j: (0, 0, i))
    return pl.pallas_call(
        functools.partial(_peer_kernel, heads=heads),
        grid=(t // tt, n_exp // et),
        in_specs=[pl.BlockSpec((tt, d), lambda i, j: (i, 0)),
                  pl.BlockSpec((et, d), lambda i, j: (j, 0)),
                  pl.BlockSpec((et, d), lambda i, j: (j, 0)),
                  mspec, mspec, mspec, mspec],
        out_specs=pl.BlockSpec((tt, d), lambda i, j: (i, 0)),
        out_shape=jax.ShapeDtypeStruct((t, d), F32),
        scratch_shapes=[pltpu.VMEM((et, tt), BF16)],
        compiler_params=_params("parallel", "arbitrary"), name="peer_dense")(h, expert_u, expert_v, n1, r2, e1, e2)


def kernel(x, norm_mix_gain, w_in, b_forget, w_branch_a, w_branch_b, w_out, norm_ffn_gain, w_query,
           sub_keys, expert_u, expert_v, norm_final_gain):
    batch, seq, d = x.shape
    t = batch * seq
    depth = w_in.shape[0]
    sb_width = w_branch_a.shape[1]
    fox_width = w_branch_b.shape[1]
    sb_heads = sb_width // HEAD_DIM
    fox_heads = fox_width // HEAD_DIM
    qkv_width = 3 * sb_width + 3 * fox_width
    gate0 = qkv_width + fox_heads
    xt = x.reshape(t, d)

    for layer in range(depth):
        h = _rms_norm(xt, norm_mix_gain[layer])
        w_l = w_in[layer]
        qkv = _matmul(h, w_l, 0, qkv_width, BF16)
        fslab = _matmul(h, w_l, qkv_width, LANES, F32, tn=LANES)
        gates = _matmul(h, w_l[:, gate0:gate0 + 2 * d], 0, 2 * d, BF16, epilogue="sigmoid")

        bias_row = jnp.zeros((1, LANES), F32).at[0, :fox_heads].set(b_forget[layer])
        f_cum = _forget_cumsum(fslab, bias_row, batch, seq)
        f_rows = (f_cum.reshape(batch, seq, LANES)[:, :, :fox_heads].transpose(0, 2, 1)
                  .reshape(batch, fox_heads, seq // ATT_BLOCK, ATT_BLOCK))

        o_a = _attention(qkv, 0, sb_heads, batch, seq)
        o_b = _attention(qkv, 3 * sb_heads, fox_heads, batch, seq, forget=(f_cum, f_rows))
        merged = _merge(o_a, o_b, w_branch_a[layer], w_branch_b[layer], gates, d)
        x1 = _matmul(merged, w_out[layer], 0, d, F32, epilogue="residual", res=xt)

        h2 = _rms_norm(x1, norm_ffn_gain[layer])
        q = _matmul(h2, w_query[layer], 0, w_query.shape[2], BF16)
        n1, r2, e1, e2 = _peer_select(q, sub_keys[layer])
        peer = _peer_dense(h2, expert_u[layer], expert_v[layer], n1, r2, e1, e2)
        if layer + 1 < depth:
            xt = x1 + peer
    out = _rms_norm(x1, norm_final_gain, res=peer, out_dtype=x.dtype)
    return out.reshape(batch, seq, d)
```
